```python
import math
import jax, jax.numpy as jnp
from jax import lax
import numpy as np

D_MODEL = 1024
BATCH = 16
SEQ = 2048
DEPTH = 4
DEC_BATCH = 8
DEC_SEQ = 8192
PAST_LEN = 128

D_MIX = D_MODEL
D_SSD = D_MIX // 2
D_CONV = D_MIX // 4
D_ATT = D_MIX - D_SSD - D_CONV
SSD_HEAD_DIM = 64
SSD_HEADS = D_SSD // SSD_HEAD_DIM
SSD_GROUPS = 2
SSD_HPG = SSD_HEADS // SSD_GROUPS
SSD_STATE = 64
SSD_CONV_W = 4
SSD_CHUNK = 128
SSD_XBC = D_SSD + 2 * SSD_GROUPS * SSD_STATE
CM_KERNEL = 31
DA_HEAD_DIM = 64
DA_HEADS = D_ATT // DA_HEAD_DIM
DA_QK_DIM = DA_HEAD_DIM // 2
Q_BLOCK = 128
REL_BUCKETS = 32
REL_MAX_DIST = 128
N_EXPERTS = 16
EXPERT_FF = 2048
EC_CAPACITY = 2
ALPHA = (2 * DEPTH) ** 0.25
BETA = (8 * DEPTH) ** -0.25
EPS = 1e-5
D_IN = D_SSD + SSD_XBC + 2 * SSD_HEADS + 2 * D_CONV + 3 * D_ATT

kernel_name = "hybrid_bidir_ssd_conformer_diffattn_ec_moe"


def _split_points():
    sizes = [D_SSD, SSD_XBC, 2 * SSD_HEADS, 2 * D_CONV, D_ATT, D_ATT, D_ATT]
    return [int(s) for s in np.cumsum(sizes)[:-1]]


def _layernorm(x, g, b):
    x32 = x.astype(jnp.float32)
    mu = jnp.mean(x32, axis=-1, keepdims=True)
    var = jnp.mean(jnp.square(x32 - mu), axis=-1, keepdims=True)
    y = (x32 - mu) * lax.rsqrt(var + EPS) * g.astype(jnp.float32) + b.astype(jnp.float32)
    return y.astype(x.dtype)


def _dwconv(u, w, bias, pad):
    out = lax.conv_general_dilated(u, w[:, None, :].astype(u.dtype), window_strides=(1,), padding=[pad],
                                   dimension_numbers=('NWC', 'WIO', 'NWC'), feature_group_count=u.shape[-1])
    return out + bias.astype(u.dtype)


def _ssd_chunked(x, dt, a, b, c):
    bz, L = x.shape[0], x.shape[1]
    nc = L // SSD_CHUNK
    x = x.reshape(bz, nc, SSD_CHUNK, *x.shape[2:])
    dt = dt.reshape(bz, nc, SSD_CHUNK, *dt.shape[2:])
    b = b.reshape(bz, nc, SSD_CHUNK, *b.shape[2:])
    c = c.reshape(bz, nc, SSD_CHUNK, *c.shape[2:])
    a_cs = jnp.cumsum(dt * a, axis=2)
    dtx = x * dt[..., None]
    lower = jnp.tril(jnp.ones((SSD_CHUNK, SSD_CHUNK), dtype=bool))
    seg = a_cs[:, :, :, None] - a_cs[:, :, None, :]
    decay = jnp.exp(jnp.where(lower[:, :, None, None], seg, -jnp.inf))
    cb = jnp.einsum('bcign,bcjgn->bcijg', c, b)
    y_diag = jnp.einsum('bcijgr,bcjgrp->bcigrp', cb[..., None] * decay, dtx)
    decay_end = jnp.exp(a_cs[:, :, -1:] - a_cs)
    states = jnp.einsum('bcjgn,bcjgr,bcjgrp->bcgrpn', b, decay_end, dtx)
    chunk_decay = jnp.exp(a_cs[:, :, -1])

    def step(h, inp):
        st, dec = inp
        return h * dec[..., None, None] + st, h

    h0 = jnp.zeros_like(states[:, 0])
    _, h_in = lax.scan(step, h0, (jnp.moveaxis(states, 1, 0), jnp.moveaxis(chunk_decay, 1, 0)))
    h_in = jnp.moveaxis(h_in, 0, 1)
    y_off = jnp.einsum('bcign,bcgrpn,bcigr->bcigrp', c, h_in, jnp.exp(a_cs))
    return (y_diag + y_off).reshape(bz, L, *x.shape[3:])


def _ssd_mixer(z, xbc, dt_raw, conv_w, conv_b, dt_bias, a_log, d_skip, norm_g):
    bz, L, _ = xbc.shape
    left = SSD_CONV_W // 2
    xbc = jax.nn.silu(_dwconv(xbc, conv_w, conv_b, (left, SSD_CONV_W - 1 - left)))
    gn = SSD_GROUPS * SSD_STATE
    x = xbc[..., :D_SSD].reshape(bz, L, SSD_GROUPS, SSD_HPG, SSD_HEAD_DIM).astype(jnp.float32)
    b = xbc[..., D_SSD:D_SSD + gn].reshape(bz, L, SSD_GROUPS, SSD_STATE).astype(jnp.float32)
    c = xbc[..., D_SSD + gn:].reshape(bz, L, SSD_GROUPS, SSD_STATE).astype(jnp.float32)
    dt = jax.nn.softplus(dt_raw.astype(jnp.float32).reshape(bz, L, 2, SSD_GROUPS, SSD_HPG)
                         + dt_bias.astype(jnp.float32).reshape(2, SSD_GROUPS, SSD_HPG))
    a = -jnp.exp(a_log.astype(jnp.float32)).reshape(2, SSD_GROUPS, SSD_HPG)
    flip = lambda t: t[:, ::-1]
    y_f = _ssd_chunked(x, dt[:, :, 0], a[0], b, c)
    y_b = flip(_ssd_chunked(flip(x), flip(dt[:, :, 1]), a[1], flip(b), flip(c)))
    y = y_f + y_b + x * d_skip.astype(jnp.float32).reshape(SSD_GROUPS, SSD_HPG)[..., None]
    y = y.reshape(bz, L, D_SSD) * jax.nn.silu(z.astype(jnp.float32))
    yg = y.reshape(bz, L, SSD_GROUPS, D_SSD // SSD_GROUPS)
    yg = yg * lax.rsqrt(jnp.mean(jnp.square(yg), axis=-1, keepdims=True) + EPS)
    y = yg.reshape(bz, L, D_SSD) * norm_g.astype(jnp.float32)
    return y.astype(z.dtype)


def _conv_module(glu_in, dw_w, dw_b, ln_g, ln_b):
    a, g = jnp.split(glu_in, 2, axis=-1)
    u = a * jax.nn.sigmoid(g)
    half = CM_KERNEL // 2
    u = _dwconv(u, dw_w, dw_b, (half, half))
    return jax.nn.silu(_layernorm(u, ln_g, ln_b))


def _t5_bucket(rel):
    half = REL_BUCKETS // 2
    exact = half // 2
    base = jnp.where(rel > 0, half, 0)
    n = jnp.abs(rel)
    nf = jnp.maximum(n, 1).astype(jnp.float32)
    large = exact + (jnp.log(nf / exact) / math.log(REL_MAX_DIST / exact) * (half - exact)).astype(jnp.int32)
    large = jnp.minimum(large, half - 1)
    return base + jnp.where(n < exact, n, large)


def _diff_attention(q, k, v, rel_bias, lam_p, subln_g, lam_init):
    bz, L, _ = q.shape
    q = q.reshape(bz, L, DA_HEADS, 2, DA_QK_DIM)
    k = k.reshape(bz, L, DA_HEADS, 2, DA_QK_DIM)
    v = v.reshape(bz, L, DA_HEADS, DA_HEAD_DIM)
    k1, k2 = k[..., 0, :], k[..., 1, :]
    lp = lam_p.astype(jnp.float32)
    lam = jnp.exp(jnp.sum(lp[0] * lp[1])) - jnp.exp(jnp.sum(lp[2] * lp[3])) + lam_init
    nb = L // Q_BLOCK
    qb = jnp.moveaxis(q.reshape(bz, nb, Q_BLOCK, DA_HEADS, 2, DA_QK_DIM), 1, 0)
    starts = jnp.arange(nb, dtype=jnp.int32) * Q_BLOCK
    k_pos = jnp.arange(L, dtype=jnp.int32)
    scale = DA_QK_DIM ** -0.5

    def block(args):
        qblk, q0 = args
        q_pos = q0 + jnp.arange(Q_BLOCK, dtype=jnp.int32)
        bias = jnp.transpose(rel_bias[_t5_bucket(k_pos[None, :] - q_pos[:, None])], (2, 0, 1)).astype(jnp.float32)
        s1 = jnp.einsum('bqhd,bkhd->bhqk', qblk[..., 0, :], k1).astype(jnp.float32) * scale + bias
        s2 = jnp.einsum('bqhd,bkhd->bhqk', qblk[..., 1, :], k2).astype(jnp.float32) * scale + bias
        p = jax.nn.softmax(s1, axis=-1) - lam * jax.nn.softmax(s2, axis=-1)
        return jnp.einsum('bhqk,bkhe->bqhe', p.astype(v.dtype), v)

    o = lax.map(block, (qb, starts))
    o = jnp.moveaxis(o, 0, 1).reshape(bz, L, DA_HEADS, DA_HEAD_DIM).astype(jnp.float32)
    o = o * lax.rsqrt(jnp.mean(jnp.square(o), axis=-1, keepdims=True) + EPS) * subln_g.astype(jnp.float32)
    o = o * (1.0 - lam_init)
    return o.reshape(bz, L, D_ATT).astype(q.dtype)


def _expert_choice(h, w_router, w_gate, w_up, w_down):
    T = h.shape[0]
    cap = max(1, EC_CAPACITY * T // N_EXPERTS)
    aff = jax.nn.softmax((h @ w_router).astype(jnp.float32), axis=-1)
    gates, idx = lax.top_k(aff.T, cap)
    xs = h[idx]
    hid = jax.nn.silu(jnp.einsum('ecd,edf->ecf', xs, w_gate)) * jnp.einsum('ecd,edf->ecf', xs, w_up)
    out = jnp.einsum('ecf,efd->ecd', hid, w_down) * gates[..., None].astype(h.dtype)
    return jnp.zeros_like(h).at[idx.reshape(-1)].add(out.reshape(-1, h.shape[-1]))


def _trunk(x, c, w_ada, b_ada, w_in, ssd_conv_w, ssd_conv_b, ssd_dt_bias, ssd_a_log, ssd_d, ssd_norm_g,
           cm_dw_w, cm_dw_b, cm_ln_g, cm_ln_b, da_lambda, da_subln_g, rel_bias, w_out, ln_g, ln_b,
           w_router, w_gate, w_up, w_down):
    c_act = jax.nn.silu(c)
    splits = _split_points()
    for l in range(DEPTH):
        mod = c_act @ w_ada[l] + b_ada[l]
        sh1, sc1, g1, sh2, sc2, g2 = jnp.split(mod[:, None, :], 6, axis=-1)
        h = x * (1 + sc1) + sh1
        z, xbc, dt_raw, glu_in, q, k, v = jnp.split(h @ w_in[l], splits, axis=-1)
        y_ssd = _ssd_mixer(z, xbc, dt_raw, ssd_conv_w[l], ssd_conv_b[l], ssd_dt_bias[l], ssd_a_log[l],
                           ssd_d[l], ssd_norm_g[l])
        y_conv = _conv_module(glu_in, cm_dw_w[l], cm_dw_b[l], cm_ln_g[l], cm_ln_b[l])
        lam_init = 0.8 - 0.6 * math.exp(-0.3 * l)
        y_att = _diff_attention(q, k, v, rel_bias, da_lambda[l], da_subln_g[l], lam_init)
        mix = jnp.concatenate([y_ssd, y_conv, y_att], axis=-1) @ w_out[l]
        x = _layernorm(ALPHA * x + (1 + g1) * mix, ln_g[l, 0], ln_b[l, 0])
        h = x * (1 + sc2) + sh2
        ffn = _expert_choice(h.reshape(-1, D_MODEL), w_router[l], w_gate[l], w_up[l], w_down[l]).reshape(x.shape)
        x = _layernorm(ALPHA * x + (1 + g2) * ffn, ln_g[l, 1], ln_b[l, 1])
    return x


def setup_inputs(seed: int = 0) -> dict:
    key = jax.random.key(seed)
    ks = jax.random.split(key, 32)
    f32 = jnp.float32
    nrm = lambda k, shape, s: jax.random.normal(k, shape, f32) * s
    u_dt = jax.random.uniform(ks[8], (DEPTH, 2, SSD_HEADS), f32)
    dt0 = jnp.exp(u_dt * (math.log(0.1) - math.log(0.001)) + math.log(0.001))
    return {
        "x_prompt": nrm(ks[0], (BATCH, SEQ, D_MODEL), 1.0),
        "x_sample": nrm(ks[1], (DEC_BATCH, DEC_SEQ, D_MODEL), 1.0),
        "c_prompt": nrm(ks[2], (BATCH, D_MODEL), 1.0),
        "c_sample": nrm(ks[3], (DEC_BATCH, D_MODEL), 1.0),
        "w_ada": nrm(ks[4], (DEPTH, D_MODEL, 6 * D_MODEL), 0.2 * D_MODEL ** -0.5),
        "b_ada": nrm(ks[5], (DEPTH, 6 * D_MODEL), 0.02),
        "w_in": nrm(ks[6], (DEPTH, D_MODEL, D_IN), D_MODEL ** -0.5),
        "ssd_conv_w": nrm(ks[7], (DEPTH, SSD_CONV_W, SSD_XBC), SSD_CONV_W ** -0.5),
        "ssd_conv_b": nrm(ks[9], (DEPTH, SSD_XBC), 0.02),
        "ssd_dt_bias": dt0 + jnp.log(-jnp.expm1(-dt0)),
        "ssd_a_log": jnp.log(jax.random.uniform(ks[10], (DEPTH, 2, SSD_HEADS), f32, 1.0, 16.0)),
        "ssd_d": 1.0 + nrm(ks[11], (DEPTH, SSD_HEADS), 0.02),
        "ssd_norm_g": 1.0 + nrm(ks[12], (DEPTH, D_SSD), 0.02),
        "cm_dw_w": nrm(ks[13], (DEPTH, CM_KERNEL, D_CONV), CM_KERNEL ** -0.5),
        "cm_dw_b": nrm(ks[14], (DEPTH, D_CONV), 0.02),
        "cm_ln_g": 1.0 + nrm(ks[15], (DEPTH, D_CONV), 0.02),
        "cm_ln_b": nrm(ks[16], (DEPTH, D_CONV), 0.02),
        "da_lambda": nrm(ks[17], (DEPTH, 4, DA_QK_DIM), 0.1),
        "da_subln_g": 1.0 + nrm(ks[18], (DEPTH, DA_HEAD_DIM), 0.02),
        "rel_bias": nrm(ks[19], (REL_BUCKETS, DA_HEADS), 0.5),
        "w_out": nrm(ks[20], (DEPTH, D_MIX, D_MODEL), BETA * D_MIX ** -0.5),
        "ln_g": 1.0 + nrm(ks[21], (DEPTH, 2, D_MODEL), 0.02),
        "ln_b": nrm(ks[22], (DEPTH, 2, D_MODEL), 0.02),
        "w_router": nrm(ks[23], (DEPTH, D_MODEL, N_EXPERTS), D_MODEL ** -0.5),
        "w_gate": nrm(ks[24], (DEPTH, N_EXPERTS, D_MODEL, EXPERT_FF), D_MODEL ** -0.5),
        "w_up": nrm(ks[25], (DEPTH, N_EXPERTS, D_MODEL, EXPERT_FF), D_MODEL ** -0.5),
        "w_down": nrm(ks[26], (DEPTH, N_EXPERTS, EXPERT_FF, D_MODEL), BETA * EXPERT_FF ** -0.5),
    }


def reference(x_prompt, x_sample, c_prompt, c_sample, w_ada, b_ada, w_in, ssd_conv_w, ssd_conv_b,
              ssd_dt_bias, ssd_a_log, ssd_d, ssd_norm_g, cm_dw_w, cm_dw_b, cm_ln_g, cm_ln_b,
              da_lambda, da_subln_g, rel_bias, w_out, ln_g, ln_b, w_router, w_gate, w_up, w_down):
    y_prompt = _trunk(x_prompt, c_prompt, w_ada, b_ada, w_in, ssd_conv_w, ssd_conv_b, ssd_dt_bias, ssd_a_log,
                      ssd_d, ssd_norm_g, cm_dw_w, cm_dw_b, cm_ln_g, cm_ln_b, da_lambda, da_subln_g, rel_bias,
                      w_out, ln_g, ln_b, w_router, w_gate, w_up, w_down)
    y_sample = _trunk(x_sample, c_sample, w_ada, b_ada, w_in, ssd_conv_w, ssd_conv_b, ssd_dt_bias, ssd_a_log,
                      ssd_d, ssd_norm_g, cm_dw_w, cm_dw_b, cm_ln_g, cm_ln_b, da_lambda, da_subln_g, rel_bias,
                      w_out, ln_g, ln_b, w_router, w_gate, w_up, w_down)
    return (y_prompt, y_sample)
```

```python
import functools
import math

import jax
import jax.numpy as jnp
import numpy as np
from jax import lax
from jax.experimental import pallas as pl
from jax.experimental.pallas import tpu as pltpu

F32 = jnp.float32
BF16 = jnp.bfloat16
I32 = jnp.int32

D_MODEL = 1024
DEPTH = 4
D_SSD = 512
D_CONV = 256
D_ATT = 256
SSD_HEAD_DIM = 64
SSD_HEADS = 8
SSD_GROUPS = 2
SSD_HPG = 4
SSD_STATE = 64
SSD_CONV_W = 4
SSD_CHUNK = 128
SSD_XBC = 768
CM_KERNEL = 31
DA_HEAD_DIM = 64
DA_HEADS = 4
DA_QK_DIM = 32
REL_BUCKETS = 32
REL_MAX_DIST = 128
N_EXPERTS = 16
EXPERT_FF = 2048
EC_CAPACITY = 2
ALPHA = (2 * DEPTH) ** 0.25
EPS = 1e-5

LANES = 128
SUBLANES = 8
BF16_ROWS = 16

ZX_W = D_SSD + SSD_XBC + LANES
DT_OFF = D_SSD + SSD_XBC
H2_W = D_MODEL + LANES
LOG2E = 1.4426950408889634
NEG_BIG = -1e30

ROW_TILE = 512
ATT_TILE = 512
COMB_TILE = 256
COMB_WIN = 64
FF_CHUNK = 512
VMEM_LIMIT = 56 * 1024 * 1024


def _cp(sem, vmem=VMEM_LIMIT):
    return pltpu.CompilerParams(dimension_semantics=sem, vmem_limit_bytes=vmem)


def _sigmoid(x):
    return 1.0 / (1.0 + jnp.exp(-x))


def _silu(x):
    return x * _sigmoid(x)


def _dot(a, b):
    return jnp.dot(a, b, preferred_element_type=F32)


def _dot_nt(a, b):
    return lax.dot_general(a, b, (((1,), (1,)), ((), ())), preferred_element_type=F32)


def _layernorm(x, g, b):
    mu = jnp.mean(x, axis=-1, keepdims=True)
    xc = x - mu
    var = jnp.mean(xc * xc, axis=-1, keepdims=True)
    return xc * lax.rsqrt(var + EPS) * g + b


def _ada_kernel(c_ref, w_ref, b_ref, o_ref):
    c = c_ref[...]
    o_ref[0] = _dot(_silu(c).astype(BF16), w_ref[0].astype(BF16)) + b_ref[0]


def _ada(c, w_ada, b_ada):
    bsz = c.shape[0]
    return pl.pallas_call(
        _ada_kernel,
        out_shape=jax.ShapeDtypeStruct((DEPTH, bsz, 6 * D_MODEL), F32),
        grid=(DEPTH, 6),
        in_specs=[
            pl.BlockSpec((bsz, D_MODEL), lambda l, j: (0, 0)),
            pl.BlockSpec((1, D_MODEL, D_MODEL), lambda l, j: (l, 0, j)),
            pl.BlockSpec((1, 1, D_MODEL), lambda l, j: (l, 0, j)),
        ],
        out_specs=pl.BlockSpec((1, bsz, D_MODEL), lambda l, j: (l, 0, j)),
        compiler_params=_cp(("parallel", "parallel")),
        name="ada",
    )(c, w_ada, b_ada.reshape(DEPTH, 1, 6 * D_MODEL))


def _in_kernel(x_ref, mod_ref, wssd_ref, wglu_ref, wq_ref, wk_ref, wvt_ref,
               zx_ref, glu_ref, q_ref, k_ref, vt_ref, *, qscale):
    m = mod_ref[0]
    h = x_ref[0] * (1.0 + m[:, D_MODEL:2 * D_MODEL]) + m[:, 0:D_MODEL]
    hb = h.astype(BF16)
    zx_ref[0] = _dot(hb, wssd_ref[0])
    glu_ref[0] = _dot(hb, wglu_ref[0])
    q_ref[0] = (_dot(hb, wq_ref[0]) * qscale).astype(BF16)
    k_ref[0] = _dot(hb, wk_ref[0]).astype(BF16)
    vt_ref[0] = _dot_nt(wvt_ref[0], hb).astype(BF16)


def _in_proj(x, mod_l, wp, l):
    bsz, seq, _ = x.shape
    tm = min(ROW_TILE, seq)
    wspec = lambda shp: pl.BlockSpec((1,) + shp, lambda b, i: (l, 0, 0))
    return pl.pallas_call(
        functools.partial(_in_kernel, qscale=DA_QK_DIM ** -0.5 * LOG2E),
        out_shape=(
            jax.ShapeDtypeStruct((bsz, seq, ZX_W), F32),
            jax.ShapeDtypeStruct((bsz, seq, 2 * D_CONV), F32),
            jax.ShapeDtypeStruct((bsz, seq, D_ATT), BF16),
            jax.ShapeDtypeStruct((bsz, seq, D_ATT), BF16),
            jax.ShapeDtypeStruct((bsz, D_ATT, seq), BF16),
        ),
        grid=(bsz, seq // tm),
        in_specs=[
            pl.BlockSpec((1, tm, D_MODEL), lambda b, i: (b, i, 0)),
            pl.BlockSpec((1, 1, 6 * D_MODEL), lambda b, i: (b, 0, 0)),
            wspec((D_MODEL, ZX_W)), wspec((D_MODEL, 2 * D_CONV)),
            wspec((D_MODEL, D_ATT)), wspec((D_MODEL, D_ATT)), wspec((D_ATT, D_MODEL)),
        ],
        out_specs=(
            pl.BlockSpec((1, tm, ZX_W), lambda b, i: (b, i, 0)),
            pl.BlockSpec((1, tm, 2 * D_CONV), lambda b, i: (b, i, 0)),
            pl.BlockSpec((1, tm, D_ATT), lambda b, i: (b, i, 0)),
            pl.BlockSpec((1, tm, D_ATT), lambda b, i: (b, i, 0)),
            pl.BlockSpec((1, D_ATT, tm), lambda b, i: (b, 0, i)),
        ),
        compiler_params=_cp(("parallel", "parallel")),
        name="in_proj",
    )(x, mod_l, wp["w_ssd"], wp["w_glu"], wp["w_q"], wp["w_k"], wp["w_vt"])


def _halo_specs(tm, halo, width, seq):
    nh = tm // halo
    last = seq // halo - 1
    return [
        pl.BlockSpec((1, tm, width), lambda b, i: (b, i, 0)),
        pl.BlockSpec((1, halo, width), lambda b, i: (b, jnp.maximum(i * nh - 1, 0), 0)),
        pl.BlockSpec((1, halo, width), lambda b, i: (b, jnp.minimum((i + 1) * nh, last), 0)),
    ]


def _ssdconv_kernel(main_ref, prev_ref, next_ref, w_ref, b_ref, o_ref, ext_ref, *, tm):
    i = pl.program_id(1)
    n = pl.num_programs(1)
    h = SUBLANES
    ext_ref[0:h] = jnp.where(i > 0, prev_ref[0, :, D_SSD:DT_OFF], 0.0)
    ext_ref[h:h + tm] = main_ref[0, :, D_SSD:DT_OFF]
    ext_ref[h + tm:2 * h + tm] = jnp.where(i < n - 1, next_ref[0, :, D_SSD:DT_OFF], 0.0)
    left = SSD_CONV_W // 2
    acc = jnp.zeros((tm, SSD_XBC), F32) + b_ref[0]
    for w in range(SSD_CONV_W):
        acc = acc + ext_ref[pl.ds(h - left + w, tm), :] * w_ref[0, w:w + 1, :]
    o_ref[0] = _silu(acc)


def _ssd_conv(zx, conv_w, conv_b, l):
    bsz, seq, _ = zx.shape
    tm = min(ROW_TILE, seq)
    return pl.pallas_call(
        functools.partial(_ssdconv_kernel, tm=tm),
        out_shape=jax.ShapeDtypeStruct((bsz, seq, SSD_XBC), F32),
        grid=(bsz, seq // tm),
        in_specs=_halo_specs(tm, SUBLANES, ZX_W, seq) + [
            pl.BlockSpec((1, SSD_CONV_W, SSD_XBC), lambda b, i: (l, 0, 0)),
            pl.BlockSpec((1, 1, SSD_XBC), lambda b, i: (l, 0, 0)),
        ],
        out_specs=pl.BlockSpec((1, tm, SSD_XBC), lambda b, i: (b, i, 0)),
        scratch_shapes=[pltpu.VMEM((tm + 2 * SUBLANES, SSD_XBC), F32)],
        compiler_params=_cp(("parallel", "parallel")),
        name="ssd_conv",
    )(zx, zx, zx, conv_w, conv_b.reshape(DEPTH, 1, SSD_XBC))


def _ssd_kernel(xf_ref, dtf_ref, xb_ref, dtb_ref, dtbias_ref, alog_ref, yf_ref, yb_ref, st_ref):
    c = pl.program_id(1)

    @pl.when(c == 0)
    def _():
        st_ref[...] = jnp.zeros_like(st_ref)

    q = SSD_CHUNK
    ii = lax.broadcasted_iota(I32, (q, q), 0)
    jj = lax.broadcasted_iota(I32, (q, q), 1)
    a_neg = -jnp.exp(alog_ref[0])
    gn = SSD_GROUPS * SSD_STATE
    for d, (x_ref, dt_ref, y_ref) in enumerate(((xf_ref, dtf_ref, yf_ref), (xb_ref, dtb_ref, yb_ref))):
        keep = (jj <= ii) if d == 0 else (jj >= ii)
        xbc = x_ref[0]
        raw = dt_ref[0] + dtbias_ref[0]
        dt = jnp.maximum(raw, 0.0) + jnp.log(1.0 + jnp.exp(-jnp.abs(raw)))
        a_dt = dt * a_neg
        acs = jnp.dot(keep.astype(F32), a_dt, preferred_element_type=F32, precision=lax.Precision.HIGHEST)
        acs_t = acs.T
        last = acs[q - 1:q, :] if d == 0 else acs[0:1, :]
        e_acs = jnp.exp(acs)
        d_end = jnp.exp(last - acs)
        c_dec = jnp.exp(last)
        ys = []
        for g in range(SSD_GROUPS):
            bg = xbc[:, D_SSD + g * SSD_STATE:D_SSD + (g + 1) * SSD_STATE]
            cg = xbc[:, D_SSD + gn + g * SSD_STATE:D_SSD + gn + (g + 1) * SSD_STATE]
            bt = bg.T.astype(BF16)
            cgb = cg.astype(BF16)
            cb = _dot(cgb, bt)
            for r in range(SSD_HPG):
                hd = g * SSD_HPG + r
                ln = d * SSD_HEADS + hd
                seg = acs[:, ln:ln + 1] - acs_t[ln:ln + 1, :]
                decay = jnp.exp(jnp.where(keep, seg, NEG_BIG))
                mmat = (cb * decay).astype(BF16)
                xh = xbc[:, hd * SSD_HEAD_DIM:(hd + 1) * SSD_HEAD_DIM]
                dtx = xh * dt[:, ln:ln + 1]
                st = st_ref[ln]
                y = _dot(mmat, dtx.astype(BF16)) + _dot(cgb, st.astype(BF16)) * e_acs[:, ln:ln + 1]
                upd = _dot(bt, (dtx * d_end[:, ln:ln + 1]).astype(BF16))
                st_ref[ln] = st * c_dec[:, ln:ln + 1] + upd
                ys.append(y)
        y_ref[0] = jnp.concatenate(ys, axis=1)


def _ssd_scan(xbc_act, zx, dt_bias, a_log, l):
    bsz, seq, _ = xbc_act.shape
    nc = seq // SSD_CHUNK
    dt_blk = DT_OFF // LANES
    pspec = pl.BlockSpec((1, 1, LANES), lambda b, c: (l, 0, 0))
    return pl.pallas_call(
        _ssd_kernel,
        out_shape=(jax.ShapeDtypeStruct((bsz, seq, D_SSD), F32),) * 2,
        grid=(bsz, nc),
        in_specs=[
            pl.BlockSpec((1, SSD_CHUNK, SSD_XBC), lambda b, c: (b, c, 0)),
            pl.BlockSpec((1, SSD_CHUNK, LANES), lambda b, c: (b, c, dt_blk)),
            pl.BlockSpec((1, SSD_CHUNK, SSD_XBC), lambda b, c: (b, nc - 1 - c, 0)),
            pl.BlockSpec((1, SSD_CHUNK, LANES), lambda b, c: (b, nc - 1 - c, dt_blk)),
            pspec, pspec,
        ],
        out_specs=(
            pl.BlockSpec((1, SSD_CHUNK, D_SSD), lambda b, c: (b, c, 0)),
            pl.BlockSpec((1, SSD_CHUNK, D_SSD), lambda b, c: (b, nc - 1 - c, 0)),
        ),
        scratch_shapes=[pltpu.VMEM((2 * SSD_HEADS, SSD_STATE, SSD_HEAD_DIM), F32)],
        compiler_params=_cp(("parallel", "arbitrary")),
        name="ssd_scan",
    )(xbc_act, zx, xbc_act, zx, dt_bias, a_log)


def _ssdfin_kernel(yf_ref, yb_ref, x_ref, zx_ref, d_ref, g_ref, o_ref):
    x = x_ref[0, :, 0:D_SSD]
    z = zx_ref[0, :, 0:D_SSD]
    y = (yf_ref[0] + yb_ref[0] + x * d_ref[0]) * _silu(z)
    gw = D_SSD // SSD_GROUPS
    outs = []
    for g in range(SSD_GROUPS):
        yg = y[:, g * gw:(g + 1) * gw]
        ms = jnp.mean(yg * yg, axis=-1, keepdims=True)
        outs.append(yg * lax.rsqrt(ms + EPS))
    o_ref[0] = (jnp.concatenate(outs, axis=1) * g_ref[0]).astype(BF16)


def _ssd_finish(yf, yb, xbc_act, zx, d_exp, norm_g, l):
    bsz, seq, _ = yf.shape
    tm = min(ROW_TILE, seq)
    row = lambda w: pl.BlockSpec((1, tm, w), lambda b, i: (b, i, 0))
    par = pl.BlockSpec((1, 1, D_SSD), lambda b, i: (l, 0, 0))
    return pl.pallas_call(
        _ssdfin_kernel,
        out_shape=jax.ShapeDtypeStruct((bsz, seq, D_SSD), BF16),
        grid=(bsz, seq // tm),
        in_specs=[row(D_SSD), row(D_SSD), row(SSD_XBC), row(ZX_W), par, par],
        out_specs=row(D_SSD),
        compiler_params=_cp(("parallel", "parallel")),
        name="ssd_finish",
    )(yf, yb, xbc_act, zx, d_exp, norm_g)


def _cm_kernel(main_ref, prev_ref, next_ref, w_ref, b_ref, g_ref, beta_ref, o_ref, ext_ref, *, tm):
    i = pl.program_id(1)
    n = pl.num_programs(1)
    h = BF16_ROWS
    glu = lambda v: v[:, 0:D_CONV] * _sigmoid(v[:, D_CONV:2 * D_CONV])
    ext_ref[0:h] = jnp.where(i > 0, glu(prev_ref[0]), 0.0)
    ext_ref[h:h + tm] = glu(main_ref[0])
    ext_ref[h + tm:2 * h + tm] = jnp.where(i < n - 1, glu(next_ref[0]), 0.0)
    half = CM_KERNEL // 2
    acc = jnp.zeros((tm, D_CONV), F32) + b_ref[0]
    for w in range(CM_KERNEL):
        acc = acc + ext_ref[pl.ds(h - half + w, tm), :] * w_ref[0, w:w + 1, :]
    o_ref[0] = _silu(_layernorm(acc, g_ref[0], beta_ref[0])).astype(BF16)


def _conv_module(glu, dw_w, dw_b, ln_g, ln_b, l):
    bsz, seq, _ = glu.shape
    tm = min(ROW_TILE, seq)
    par = pl.BlockSpec((1, 1, D_CONV), lambda b, i: (l, 0, 0))
    return pl.pallas_call(
        functools.partial(_cm_kernel, tm=tm),
        out_shape=jax.ShapeDtypeStruct((bsz, seq, D_CONV), BF16),
        grid=(bsz, seq // tm),
        in_specs=_halo_specs(tm, BF16_ROWS, 2 * D_CONV, seq) + [
            pl.BlockSpec((1, CM_KERNEL, D_CONV), lambda b, i: (l, 0, 0)), par, par, par,
        ],
        out_specs=pl.BlockSpec((1, tm, D_CONV), lambda b, i: (b, i, 0)),
        scratch_shapes=[pltpu.VMEM((tm + 2 * BF16_ROWS, D_CONV), F32)],
        compiler_params=_cp(("parallel", "parallel")),
        name="conv_module",
    )(glu, glu, glu, dw_w, dw_b.reshape(DEPTH, 1, D_CONV), ln_g.reshape(DEPTH, 1, D_CONV),
      ln_b.reshape(DEPTH, 1, D_CONV))


def _t5_bucket(rel):
    half = REL_BUCKETS // 2
    exact = half // 2
    base = jnp.where(rel > 0, half, 0)
    n = jnp.abs(rel)
    nf = jnp.maximum(n, 1).astype(jnp.float32)
    large = exact + (jnp.log(nf / exact) / math.log(REL_MAX_DIST / exact) * (half - exact)).astype(jnp.int32)
    large = jnp.minimum(large, half - 1)
    return base + jnp.where(n < exact, n, large)


def _bias_tables(rel_bias, ta):
    assert ta >= REL_MAX_DIST
    kk = jnp.arange(ta, dtype=I32)[:, None]
    qq = jnp.arange(ta, dtype=I32)[None, :]
    tabs = []
    for d in (-2, -1, 0, 1, 2):
        bk = _t5_bucket(d * ta + kk - qq)
        tabs.append(jnp.transpose(rel_bias[bk], (2, 0, 1)).astype(F32))
    return jnp.stack(tabs) * LOG2E


def _att_kernel(li_ref, q_ref, k_ref, vt_ref, bias_ref, lam_ref, sg_ref, o_ref, qm_ref, acc_ref, m_ref,
                *, tq, tk):
    kj = pl.program_id(2)
    nk = pl.num_programs(2)
    nmat = 2 * DA_HEADS
    vrows = DA_HEAD_DIM + BF16_ROWS

    @pl.when(kj == 0)
    def _():
        qv = q_ref[0]
        lane = lax.broadcasted_iota(I32, (tq, D_ATT), 1)
        for j in range(nmat):
            sel = (lane >= j * DA_QK_DIM) & (lane < (j + 1) * DA_QK_DIM)
            qm_ref[j] = jnp.where(sel, qv, jnp.zeros_like(qv))
        m_ref[...] = jnp.full_like(m_ref, NEG_BIG)
        acc_ref[...] = jnp.zeros_like(acc_ref)

    kv = k_ref[0]
    vt = vt_ref[0]
    ones = jnp.ones((BF16_ROWS, tk), BF16)
    for h in range(DA_HEADS):
        vaug = jnp.concatenate([vt[h * DA_HEAD_DIM:(h + 1) * DA_HEAD_DIM], ones], axis=0)
        bias = bias_ref[0, h]
        for half in range(2):
            j = 2 * h + half
            s = _dot_nt(kv, qm_ref[j]) + bias
            m_old = m_ref[j:j + 1, :]
            m_new = jnp.maximum(m_old, jnp.max(s, axis=0, keepdims=True))
            p = jnp.exp2(s - m_new).astype(BF16)
            acc_ref[j] = acc_ref[j] * jnp.exp2(m_old - m_new) + _dot(vaug, p)
            m_ref[j:j + 1, :] = m_new

    @pl.when(kj == nk - 1)
    def _():
        lam_init = li_ref[0]
        lp = lam_ref[0]
        lam = (jnp.exp(jnp.sum(lp[0:1] * lp[1:2], axis=-1, keepdims=True))
               - jnp.exp(jnp.sum(lp[2:3] * lp[3:4], axis=-1, keepdims=True)) + lam_init)
        outs = []
        for h in range(DA_HEADS):
            a1 = acc_ref[2 * h]
            a2 = acc_ref[2 * h + 1]
            o1 = a1[0:DA_HEAD_DIM] / a1[DA_HEAD_DIM:DA_HEAD_DIM + 1]
            o2 = a2[0:DA_HEAD_DIM] / a2[DA_HEAD_DIM:DA_HEAD_DIM + 1]
            o = o1 - lam * o2
            ms = jnp.mean(o * o, axis=0, keepdims=True)
            outs.append(o * lax.rsqrt(ms + EPS) * sg_ref[0] * (1.0 - lam_init))
        o_ref[0] = jnp.concatenate(outs, axis=0).T.astype(BF16)


def _attention(q, k, vt, bias_tabs, da_lambda, subln_g, lam_init, l):
    bsz, seq, _ = q.shape
    ta = bias_tabs.shape[-1]
    nt = seq // ta
    vrows = DA_HEAD_DIM + BF16_ROWS
    return pl.pallas_call(
        functools.partial(_att_kernel, tq=ta, tk=ta),
        out_shape=jax.ShapeDtypeStruct((bsz, seq, D_ATT), BF16),
        grid=(bsz, nt, nt),
        in_specs=[
            pl.BlockSpec(memory_space=pltpu.SMEM),
            pl.BlockSpec((1, ta, D_ATT), lambda b, i, j: (b, i, 0)),
            pl.BlockSpec((1, ta, D_ATT), lambda b, i, j: (b, j, 0)),
            pl.BlockSpec((1, D_ATT, ta), lambda b, i, j: (b, 0, j)),
            pl.BlockSpec((1, DA_HEADS, ta, ta), lambda b, i, j: (jnp.clip(j - i, -2, 2) + 2, 0, 0, 0)),
            pl.BlockSpec((1, 4, DA_QK_DIM), lambda b, i, j: (l, 0, 0)),
            pl.BlockSpec((1, DA_HEAD_DIM, 1), lambda b, i, j: (l, 0, 0)),
        ],
        out_specs=pl.BlockSpec((1, ta, D_ATT), lambda b, i, j: (b, i, 0)),
        scratch_shapes=[
            pltpu.VMEM((2 * DA_HEADS, ta, D_ATT), BF16),
            pltpu.VMEM((2 * DA_HEADS, vrows, ta), F32),
            pltpu.VMEM((2 * DA_HEADS, ta), F32),
        ],
        compiler_params=_cp(("parallel", "parallel", "arbitrary")),
        name="diff_attention",
    )(lam_init, q, k, vt, bias_tabs, da_lambda, subln_g.reshape(DEPTH, DA_HEAD_DIM, 1))


def _out_kernel(ys_ref, yc_ref, ya_ref, x_ref, mod_ref, w1_ref, w2_ref, w3_ref, g_ref, b_ref, wr_ref,
                x1_ref, h2_ref, afft_ref):
    m = mod_ref[0]
    mix = _dot(ys_ref[0], w1_ref[0]) + _dot(yc_ref[0], w2_ref[0]) + _dot(ya_ref[0], w3_ref[0])
    x1 = _layernorm(ALPHA * x_ref[0] + (1.0 + m[:, 2 * D_MODEL:3 * D_MODEL]) * mix, g_ref[0, 0:1], b_ref[0, 0:1])
    x1_ref[0] = x1
    h2 = x1 * (1.0 + m[:, 4 * D_MODEL:5 * D_MODEL]) + m[:, 3 * D_MODEL:4 * D_MODEL]
    h_hi = h2.astype(BF16)
    h_lo = (h2 - h_hi.astype(F32)).astype(BF16)
    wr = wr_ref[0]
    w_hi = wr.astype(BF16)
    w_lo = (wr - w_hi.astype(F32)).astype(BF16)
    logits = _dot(h_hi, w_hi) + _dot(h_lo, w_hi) + _dot(h_hi, w_lo)
    lane = lax.broadcasted_iota(I32, logits.shape, 1)
    logits = jnp.where(lane < N_EXPERTS, logits, NEG_BIG)
    e = jnp.exp(logits - jnp.max(logits, axis=-1, keepdims=True))
    aff = e / jnp.sum(e, axis=-1, keepdims=True)
    h2_ref[:, 0:D_MODEL] = h2
    h2_ref[:, D_MODEL:H2_W] = aff
    afft_ref[...] = aff.T[0:N_EXPERTS]


def _out_proj(y_ssd, y_conv, y_att, x, mod_l, wp, ln_g, ln_b, l):
    bsz, seq, _ = x.shape
    tm = min(ROW_TILE, seq)
    nl = seq // tm
    ntok = bsz * seq
    row = lambda w: pl.BlockSpec((1, tm, w), lambda b, i: (b, i, 0))
    wspec = lambda shp: pl.BlockSpec((1,) + shp, lambda b, i: (l, 0, 0))
    return pl.pallas_call(
        _out_kernel,
        out_shape=(
            jax.ShapeDtypeStruct((bsz, seq, D_MODEL), F32),
            jax.ShapeDtypeStruct((ntok, H2_W), F32),
            jax.ShapeDtypeStruct((N_EXPERTS, ntok), F32),
        ),
        grid=(bsz, nl),
        in_specs=[
            row(D_SSD), row(D_CONV), row(D_ATT), row(D_MODEL),
            pl.BlockSpec((1, 1, 6 * D_MODEL), lambda b, i: (b, 0, 0)),
            wspec((D_SSD, D_MODEL)), wspec((D_CONV, D_MODEL)), wspec((D_ATT, D_MODEL)),
            wspec((2, D_MODEL)), wspec((2, D_MODEL)), wspec((D_MODEL, LANES)),
        ],
        out_specs=(
            row(D_MODEL),
            pl.BlockSpec((tm, H2_W), lambda b, i: (b * nl + i, 0)),
            pl.BlockSpec((N_EXPERTS, tm), lambda b, i: (0, b * nl + i)),
        ),
        compiler_params=_cp(("parallel", "parallel")),
        name="out_proj",
    )(y_ssd, y_conv, y_att, x, mod_l, wp["w_o1"], wp["w_o2"], wp["w_o3"], ln_g, ln_b, wp["w_router"])


def _route_kernel(aff_ref, selpos_ref, rp_ref, idx_ref, msk_ref, *, cap, nr, sc):
    bits = pltpu.bitcast(aff_ref[...], I32)
    shp = bits.shape

    def count(pred):
        s = jnp.sum(pred.astype(F32), axis=1, keepdims=True)
        return jnp.sum(s, axis=2, keepdims=True)

    def tau_step(i, lo):
        cand = lo | jnp.left_shift(jnp.int32(1), 30 - i)
        return jnp.where(count(bits >= cand) >= cap, cand, lo)

    tau = lax.fori_loop(0, 31, tau_step, jnp.zeros((N_EXPERTS, 1, 1), I32))
    gt = bits > tau
    eq = bits == tau
    need = cap - count(gt)
    tok = lax.broadcasted_iota(I32, shp, 1) * LANES + lax.broadcasted_iota(I32, shp, 2)
    nbit = int(nr * LANES).bit_length()

    def tie_step(i, lo):
        cand = lo | jnp.left_shift(jnp.int32(1), nbit - 1 - i)
        return jnp.where(count(eq & (tok < cand)) < need, cand, lo)

    t_last = lax.fori_loop(0, nbit, tie_step, jnp.zeros((N_EXPERTS, 1, 1), I32))
    msk_ref[...] = jnp.where(gt | (eq & (tok <= t_last)), 1.0, 0.0)

    i0 = lax.broadcasted_iota(I32, (LANES, LANES), 0)
    i1 = lax.broadcasted_iota(I32, (LANES, LANES), 1)
    upper = jnp.where(i0 <= i1, 1.0, 0.0).astype(BF16)
    ones = jnp.ones((LANES, LANES), BF16)
    r0 = lax.broadcasted_iota(I32, (nr, nr), 0)
    r1 = lax.broadcasted_iota(I32, (nr, nr), 1)
    below = jnp.where(r1 < r0, 1.0, 0.0).astype(BF16)
    rcol = lax.broadcasted_iota(I32, (nr, 1), 0).astype(F32)

    def per_expert(e, carry):
        mk = msk_ref[e]
        mb = mk.astype(BF16)
        loc = _dot(mb, upper)
        rt = _dot(mb, ones)
        pre = _dot(below, rt.astype(BF16))
        selpos_ref[e] = jnp.where(mk > 0.0, loc + pre - 1.0, -1.0).astype(I32)
        rp_ref[e] = pre.astype(I32)
        r_incl = (pre + rt)[:, 0:1]
        r_excl = pre[:, 0:1]
        loc_t = loc.T.astype(BF16)

        def per_chunk(c, carry2):
            off = pl.multiple_of(c * sc, LANES)
            s_row = (lax.broadcasted_iota(I32, (1, sc), 1) + off).astype(F32)
            row = jnp.sum(jnp.where(r_incl <= s_row, 1.0, 0.0), axis=0, keepdims=True)
            hit = rcol == row
            before = jnp.sum(jnp.where(hit, r_excl, 0.0), axis=0, keepdims=True)
            lc = _dot(loc_t, jnp.where(hit, 1.0, 0.0).astype(BF16))
            local = jnp.sum(jnp.where(lc <= s_row - before, 1.0, 0.0), axis=0, keepdims=True)
            idx_ref[pl.ds(e, 1), pl.ds(off, sc)] = (row * LANES + local).astype(I32)
            return carry2

        return lax.fori_loop(0, cap // sc, per_chunk, carry)

    lax.fori_loop(0, N_EXPERTS, per_expert, 0)


def _route(aff_t, cap):
    ntok = aff_t.shape[1]
    nr = ntok // LANES
    sc = min(1024, cap)
    return pl.pallas_call(
        functools.partial(_route_kernel, cap=cap, nr=nr, sc=sc),
        out_shape=(
            jax.ShapeDtypeStruct((N_EXPERTS, nr, LANES), I32),
            jax.ShapeDtypeStruct((N_EXPERTS, nr, LANES), I32),
            jax.ShapeDtypeStruct((N_EXPERTS, cap), I32),
        ),
        scratch_shapes=[pltpu.VMEM((N_EXPERTS, nr, LANES), F32)],
        compiler_params=pltpu.CompilerParams(vmem_limit_bytes=VMEM_LIMIT),
        name="route",
    )(aff_t.reshape(N_EXPERTS, nr, LANES))


def _gather_copy(h2_hbm, xbuf, sem, row, slot):
    return pltpu.make_async_copy(h2_hbm.at[pl.ds(row, 1)], xbuf.at[pl.ds(slot, 1)], sem)


def _ffn_kernel(idx_ref, h2_hbm, wg_ref, wu_ref, wd_ref, y_ref, xbuf, sem, *, tmf):
    e = pl.program_id(0)

    def start(s, carry):
        _gather_copy(h2_hbm, xbuf, sem, idx_ref[0, 0, s], s).start()
        return carry

    lax.fori_loop(0, tmf, start, 0)

    def wait(s, carry):
        _gather_copy(h2_hbm, xbuf, sem, 0, s).wait()
        return carry

    lax.fori_loop(0, tmf, wait, 0)

    xb = xbuf[:, 0:D_MODEL].astype(BF16)
    aff = xbuf[:, D_MODEL:H2_W]
    lane = lax.broadcasted_iota(I32, aff.shape, 1)
    gate = jnp.sum(jnp.where(lane == e, aff, 0.0), axis=-1, keepdims=True)
    acc = jnp.zeros((tmf, D_MODEL), F32)
    for c in range(EXPERT_FF // FF_CHUNK):
        cs = slice(c * FF_CHUNK, (c + 1) * FF_CHUNK)
        hid = _silu(_dot(xb, wg_ref[0, 0, :, cs])) * _dot(xb, wu_ref[0, 0, :, cs])
        acc = acc + _dot(hid.astype(BF16), wd_ref[0, 0, cs, :])
    y_ref[...] = (acc * gate).astype(BF16)


def _expert_ffn(idx, h2ext, wp, cap, l):
    tmf = min(ROW_TILE, cap)
    nb = cap // tmf
    return pl.pallas_call(
        functools.partial(_ffn_kernel, tmf=tmf),
        out_shape=jax.ShapeDtypeStruct((N_EXPERTS * cap, D_MODEL), BF16),
        grid=(N_EXPERTS, nb),
        in_specs=[
            pl.BlockSpec((1, 1, tmf), lambda e, b: (e * nb + b, 0, 0), memory_space=pltpu.SMEM),
            pl.BlockSpec(memory_space=pl.ANY),
            pl.BlockSpec((1, 1, D_MODEL, EXPERT_FF), lambda e, b: (l, e, 0, 0)),
            pl.BlockSpec((1, 1, D_MODEL, EXPERT_FF), lambda e, b: (l, e, 0, 0)),
            pl.BlockSpec((1, 1, EXPERT_FF, D_MODEL), lambda e, b: (l, e, 0, 0)),
        ],
        out_specs=pl.BlockSpec((tmf, D_MODEL), lambda e, b: (e * nb + b, 0)),
        scratch_shapes=[pltpu.VMEM((tmf, H2_W), F32), pltpu.SemaphoreType.DMA(())],
        compiler_params=_cp(("arbitrary", "arbitrary")),
        name="expert_ffn",
    )(idx.reshape(N_EXPERTS * nb, 1, tmf), h2ext, wp["w_gate"], wp["w_up"], wp["w_down"])


def _window_copy(y_hbm, ywin, sems, start, e):
    return pltpu.make_async_copy(y_hbm.at[pl.ds(start, COMB_WIN)], ywin.at[pl.ds(e * COMB_WIN, COMB_WIN)],
                                 sems.at[e])


def _comb_kernel(base_s, npass_s, pos_ref, basev_ref, x1_ref, mod_ref, g_ref, b_ref, y_hbm, o_ref,
                 ywin, sems, acc_ref, *, cap):
    i = pl.program_id(0)
    total = N_EXPERTS * cap
    w = COMB_WIN
    pos = pos_ref[...]
    lane_e = lax.broadcasted_iota(I32, (1, N_EXPERTS), 1)
    posg = jnp.where(pos >= 0, pos + lane_e * cap, -1)
    basev = basev_ref[0]
    rep_r = lax.broadcasted_iota(I32, (N_EXPERTS, N_EXPERTS * w), 0)
    rep_c = lax.broadcasted_iota(I32, (N_EXPERTS, N_EXPERTS * w), 1)
    rep = jnp.where(rep_c // w == rep_r, 1.0, 0.0).astype(BF16)
    slot = (lax.broadcasted_iota(I32, (1, N_EXPERTS * w), 1) % w).astype(F32)
    acc_ref[...] = jnp.zeros_like(acc_ref)

    def one_pass(p, carry):
        for e in range(N_EXPERTS):
            st = jnp.minimum(base_s[i * N_EXPERTS + e] + p * w, total - w)
            _window_copy(y_hbm, ywin, sems, pl.multiple_of(st, BF16_ROWS), e).start()
        low = basev + p * w
        off = jnp.where(posg >= low, posg - jnp.minimum(low, total - w), -1)
        off = jnp.clip(off, -1, w).astype(F32).astype(BF16)
        onehot = jnp.where(_dot(off, rep) == slot, 1.0, 0.0).astype(BF16)
        for e in range(N_EXPERTS):
            _window_copy(y_hbm, ywin, sems, 0, e).wait()
        acc_ref[...] += _dot(onehot, ywin[...])
        return carry

    lax.fori_loop(0, npass_s[i], one_pass, 0)
    m = mod_ref[0]
    o_ref[...] = _layernorm(ALPHA * x1_ref[...] + (1.0 + m[:, 5 * D_MODEL:6 * D_MODEL]) * acc_ref[...],
                            g_ref[0, 1:2], b_ref[0, 1:2])


def _combine(selpos, rowpref, y, x1, mod_l, ln_g, ln_b, cap, seq, l):
    ntok = x1.shape[0]
    tt = COMB_TILE
    nt = ntok // tt
    w = COMB_WIN
    total = N_EXPERTS * cap
    rows_per_tile = tt // LANES
    lo = rowpref[:, ::rows_per_tile, 0]
    hi = jnp.concatenate([lo[:, 1:], jnp.full((N_EXPERTS, 1), cap, I32)], axis=1)
    eoff = (jnp.arange(N_EXPERTS, dtype=I32) * cap)[:, None]
    base = jnp.minimum(eoff + (lo // BF16_ROWS) * BF16_ROWS, total - w)
    npass = jnp.maximum(jnp.max((eoff + hi - base + w - 1) // w, axis=0), 1).astype(I32)
    base_t = base.T.astype(I32)
    pos_t = selpos.reshape(N_EXPERTS, ntok).T
    per_b = seq // tt
    grid_spec = pltpu.PrefetchScalarGridSpec(
        num_scalar_prefetch=2,
        grid=(nt,),
        in_specs=[
            pl.BlockSpec((tt, N_EXPERTS), lambda i, *_: (i, 0)),
            pl.BlockSpec((1, 1, N_EXPERTS), lambda i, *_: (i, 0, 0)),
            pl.BlockSpec((tt, D_MODEL), lambda i, *_: (i, 0)),
            pl.BlockSpec((1, 1, 6 * D_MODEL), lambda i, *_: (i // per_b, 0, 0)),
            pl.BlockSpec((1, 2, D_MODEL), lambda i, *_: (l, 0, 0)),
            pl.BlockSpec((1, 2, D_MODEL), lambda i, *_: (l, 0, 0)),
            pl.BlockSpec(memory_space=pl.ANY),
        ],
        out_specs=pl.BlockSpec((tt, D_MODEL), lambda i, *_: (i, 0)),
        scratch_shapes=[
            pltpu.VMEM((N_EXPERTS * w, D_MODEL), BF16),
            pltpu.SemaphoreType.DMA((N_EXPERTS,)),
            pltpu.VMEM((tt, D_MODEL), F32),
        ],
    )
    return pl.pallas_call(
        functools.partial(_comb_kernel, cap=cap),
        out_shape=jax.ShapeDtypeStruct((ntok, D_MODEL), F32),
        grid_spec=grid_spec,
        compiler_params=_cp(("arbitrary",)),
        name="combine",
    )(base_t.reshape(-1), npass, pos_t, base_t.reshape(nt, 1, N_EXPERTS), x1, mod_l, ln_g, ln_b, y)


def _prep_weights(w_in, ssd_dt_bias, ssd_a_log, ssd_d, ssd_norm_g, w_out, w_router, w_gate, w_up, w_down):
    o = 0
    parts = {}
    for name, width in (("z", D_SSD), ("xbc", SSD_XBC), ("dt", 2 * SSD_HEADS), ("glu", 2 * D_CONV),
                        ("q", D_ATT), ("k", D_ATT), ("v", D_ATT)):
        parts[name] = w_in[:, :, o:o + width]
        o += width
    pad_lanes = lambda a: jnp.pad(a, ((0, 0), (0, 0), (0, LANES - a.shape[-1])))
    return {
        "w_ssd": jnp.concatenate([parts["z"], parts["xbc"], pad_lanes(parts["dt"])], axis=-1).astype(BF16),
        "w_glu": parts["glu"].astype(BF16),
        "w_q": parts["q"].astype(BF16),
        "w_k": parts["k"].astype(BF16),
        "w_vt": jnp.swapaxes(parts["v"], 1, 2).astype(BF16),
        "dt_bias": pad_lanes(ssd_dt_bias.reshape(DEPTH, 1, 2 * SSD_HEADS)),
        "a_log": pad_lanes(ssd_a_log.reshape(DEPTH, 1, 2 * SSD_HEADS)),
        "d_exp": jnp.repeat(ssd_d, SSD_HEAD_DIM, axis=-1).reshape(DEPTH, 1, D_SSD),
        "norm_g": ssd_norm_g.reshape(DEPTH, 1, D_SSD),
        "w_o1": w_out[:, 0:D_SSD].astype(BF16),
        "w_o2": w_out[:, D_SSD:D_SSD + D_CONV].astype(BF16),
        "w_o3": w_out[:, D_SSD + D_CONV:].astype(BF16),
        "w_router": pad_lanes(w_router),
        "w_gate": w_gate.astype(BF16),
        "w_up": w_up.astype(BF16),
        "w_down": w_down.astype(BF16),
    }


def _trunk(x, c, w_ada, b_ada, wp, ssd_conv_w, ssd_conv_b, cm_dw_w, cm_dw_b, cm_ln_g, cm_ln_b,
           da_lambda, da_subln_g, rel_bias, ln_g, ln_b):
    bsz, seq, _ = x.shape
    ntok = bsz * seq
    cap = max(1, EC_CAPACITY * ntok // N_EXPERTS)
    mod = _ada(c, w_ada, b_ada)
    bias_tabs = _bias_tables(rel_bias, min(ATT_TILE, seq))
    for l in range(DEPTH):
        mod_l = mod[l].reshape(bsz, 1, 6 * D_MODEL)
        lam_init = jnp.full((1,), 0.8 - 0.6 * math.exp(-0.3 * l), F32)
        zx, glu, q, k, vt = _in_proj(x, mod_l, wp, l)
        xbc_act = _ssd_conv(zx, ssd_conv_w, ssd_conv_b, l)
        yf, yb = _ssd_scan(xbc_act, zx, wp["dt_bias"], wp["a_log"], l)
        y_ssd = _ssd_finish(yf, yb, xbc_act, zx, wp["d_exp"], wp["norm_g"], l)
        y_conv = _conv_module(glu, cm_dw_w, cm_dw_b, cm_ln_g, cm_ln_b, l)
        y_att = _attention(q, k, vt, bias_tabs, da_lambda, da_subln_g, lam_init, l)
        x1, h2ext, aff_t = _out_proj(y_ssd, y_conv, y_att, x, mod_l, wp, ln_g, ln_b, l)
        selpos, rowpref, idx = _route(aff_t, cap)
        y = _expert_ffn(idx, h2ext, wp, cap, l)
        x = _combine(selpos, rowpref, y, x1.reshape(ntok, D_MODEL), mod_l, ln_g, ln_b, cap, seq, l)
        x = x.reshape(bsz, seq, D_MODEL)
    return x


def kernel(x_prompt, x_sample, c_prompt, c_sample, w_ada, b_ada, w_in, ssd_conv_w, ssd_conv_b, ssd_dt_bias,
           ssd_a_log, ssd_d, ssd_norm_g, cm_dw_w, cm_dw_b, cm_ln_g, cm_ln_b, da_lambda, da_subln_g, rel_bias,
           w_out, ln_g, ln_b, w_router, w_gate, w_up, w_down):
    wp = _prep_weights(w_in, ssd_dt_bias, ssd_a_log, ssd_d, ssd_norm_g, w_out, w_router, w_gate, w_up, w_down)
    run = lambda x, c: _trunk(x, c, w_ada, b_ada, wp, ssd_conv_w, ssd_conv_b, cm_dw_w, cm_dw_b, cm_ln_g,
                              cm_ln_b, da_lambda, da_subln_g, rel_bias, ln_g, ln_b)
    return (run(x_prompt, c_prompt), run(x_sample, c_sample))
```
